```python
import jax, jax.numpy as jnp
from jax import lax
import numpy as np

D_MODEL = 1024
BATCH = 8
SEQ = 8192
DEPTH = 1

EXPAND = 2
D_MIX = EXPAND * D_MODEL
C_CONV = D_MIX // 2
C_SGU = D_MIX - C_CONV
CONV_GROUPS = 8
SGU_HEADS = 8
SGU_HEAD_DIM = C_SGU // SGU_HEADS
CHUNK = 128
CONV_WIDTH = 31
CONV_PAD = CONV_WIDTH // 2
D_IN = 3 * C_CONV + 3 * C_SGU
EPS = 1e-6

kernel_name = "hybrid_conformer_conv_chunked_sgu_block"


def rms_norm(x, g):
    xf = x.astype(jnp.float32)
    y = xf * lax.rsqrt(jnp.mean(xf * xf, axis=-1, keepdims=True) + EPS)
    return (y * g.astype(jnp.float32)).astype(x.dtype)


def layer_norm(x, g, b):
    xf = x.astype(jnp.float32)
    mu = jnp.mean(xf, axis=-1, keepdims=True)
    xc = xf - mu
    var = jnp.mean(xc * xc, axis=-1, keepdims=True)
    y = xc * lax.rsqrt(var + EPS)
    return (y * g.astype(jnp.float32) + b.astype(jnp.float32)).astype(x.dtype)


def depthwise_conv(x, w, b):
    y = lax.conv_general_dilated(
        x, w[:, None, :].astype(x.dtype), window_strides=(1,),
        padding=[(CONV_PAD, CONV_PAD)],
        dimension_numbers=("NWC", "WIO", "NWC"),
        feature_group_count=x.shape[-1])
    return y + b.astype(x.dtype)


def conformer_conv_branch(a_val, a_gate, conv_w, conv_b, ln_g, ln_b):
    h = a_val * jax.nn.sigmoid(a_gate)
    h = depthwise_conv(h, conv_w, conv_b)
    h = layer_norm(h, ln_g, ln_b)
    return jax.nn.silu(h)


def chunked_sgu_branch(u, v, ln_g, ln_b, w_s, b_s):
    bsz, seq, _ = v.shape
    n_chunks = seq // CHUNK
    v = layer_norm(v, ln_g, ln_b)
    v = v.reshape(bsz, n_chunks, CHUNK, SGU_HEADS, SGU_HEAD_DIM)
    mixed = jnp.einsum("hpq,bcqhd->bcphd", w_s.astype(v.dtype), v)
    mixed = mixed + jnp.transpose(b_s).astype(v.dtype)[None, None, :, :, None]
    return u * mixed.reshape(bsz, seq, C_SGU)


def setup_inputs(seed: int = 0) -> dict:
    key = jax.random.key(seed)
    ks = jax.random.split(key, 16)
    f32 = jnp.float32
    x = jax.random.normal(ks[0], (BATCH, SEQ, D_MODEL), f32)
    norm_g = 1.0 + 0.02 * jax.random.normal(ks[1], (DEPTH, D_MODEL), f32)
    w_in = jax.random.normal(ks[2], (DEPTH, D_MODEL, D_IN), f32) * D_MODEL ** -0.5
    conv_w = jax.random.normal(ks[3], (DEPTH, CONV_WIDTH, C_CONV), f32) * CONV_WIDTH ** -0.5
    conv_b = 0.02 * jax.random.normal(ks[4], (DEPTH, C_CONV), f32)
    conv_ln_g = 1.0 + 0.02 * jax.random.normal(ks[5], (DEPTH, C_CONV), f32)
    conv_ln_b = 0.02 * jax.random.normal(ks[6], (DEPTH, C_CONV), f32)
    sgu_ln_g = 1.0 + 0.02 * jax.random.normal(ks[7], (DEPTH, C_SGU), f32)
    sgu_ln_b = 0.02 * jax.random.normal(ks[8], (DEPTH, C_SGU), f32)
    w_s = jax.random.normal(ks[9], (DEPTH, SGU_HEADS, CHUNK, CHUNK), f32) * CHUNK ** -0.5
    b_s = 1.0 + 0.02 * jax.random.normal(ks[10], (DEPTH, SGU_HEADS, CHUNK), f32)
    w_out = jax.random.normal(ks[11], (DEPTH, D_MIX, D_MODEL), f32) * D_MIX ** -0.5
    final_g = 1.0 + 0.02 * jax.random.normal(ks[12], (D_MODEL,), f32)
    return {"x": x, "norm_g": norm_g, "w_in": w_in, "conv_w": conv_w, "conv_b": conv_b,
            "conv_ln_g": conv_ln_g, "conv_ln_b": conv_ln_b, "sgu_ln_g": sgu_ln_g,
            "sgu_ln_b": sgu_ln_b, "w_s": w_s, "b_s": b_s, "w_out": w_out, "final_g": final_g}


def reference(x, norm_g, w_in, conv_w, conv_b, conv_ln_g, conv_ln_b, sgu_ln_g,
              sgu_ln_b, w_s, b_s, w_out, final_g):
    split_points = [C_CONV, 2 * C_CONV, 3 * C_CONV,
                    3 * C_CONV + C_SGU, 3 * C_CONV + 2 * C_SGU]
    for l in range(DEPTH):
        h = rms_norm(x, norm_g[l])
        proj = jnp.einsum("bsd,de->bse", h, w_in[l].astype(h.dtype))
        a_val, a_gate, g_conv, u, v, g_sgu = jnp.split(proj, split_points, axis=-1)
        y_conv = conformer_conv_branch(a_val, a_gate, conv_w[l], conv_b[l],
                                       conv_ln_g[l], conv_ln_b[l]) * jax.nn.silu(g_conv)
        y_sgu = chunked_sgu_branch(u, v, sgu_ln_g[l], sgu_ln_b[l],
                                   w_s[l], b_s[l]) * jax.nn.silu(g_sgu)
        y = jnp.concatenate([y_conv, y_sgu], axis=-1)
        x = x + jnp.einsum("bse,ed->bsd", y, w_out[l].astype(y.dtype))
    return rms_norm(x, final_g)
```

```python
import functools

import jax
import jax.numpy as jnp
from jax import lax
from jax.experimental import pallas as pl
from jax.experimental.pallas import tpu as pltpu

LANES = 128
CHUNK = 128
CONV_WIDTH = 31
CONV_PAD = CONV_WIDTH // 2
HALO = 16
EPS = 1e-6
SEQ_TILE = 256
CONV_ROWS = 64
VMEM_LIMIT_BYTES = 56 * 1024 * 1024


def _rms_norm_rows(x, g):
    ms = jnp.mean(x * x, axis=-1, keepdims=True)
    return x * lax.rsqrt(ms + EPS) * g


def _layer_norm_rows(x, g, b):
    mu = jnp.mean(x, axis=-1, keepdims=True)
    xc = x - mu
    var = jnp.mean(xc * xc, axis=-1, keepdims=True)
    return xc * lax.rsqrt(var + EPS) * g + b


def _block_kernel(x_ref, xl_ref, xr_ref, ng_ref, win_ref, cw_ref, cb_ref, clg_ref,
                  clb_ref, slg_ref, slb_ref, ws_ref, bs_ref, wout_ref, fg_ref,
                  o_ref, xn_ref, h_ref, conv_ref, y_ref):
    t = x_ref.shape[0]
    c_conv = conv_ref.shape[1]
    n_groups = c_conv // LANES
    c_sgu = y_ref.shape[1] - c_conv
    j = pl.program_id(1)
    nj = pl.num_programs(1)
    f32 = jnp.float32
    bf16 = jnp.bfloat16

    ng = ng_ref[...]
    x = x_ref[...]
    xn_ref[0:HALO, :] = _rms_norm_rows(xl_ref[...], ng).astype(bf16)
    xn_ref[HALO:HALO + t, :] = _rms_norm_rows(x, ng).astype(bf16)
    xn_ref[HALO + t:HALO + t + HALO, :] = _rms_norm_rows(xr_ref[...], ng).astype(bf16)

    rows = lax.broadcasted_iota(jnp.int32, (t + 2 * HALO, 1), 0)
    inside = jnp.logical_and(jnp.logical_or(rows >= HALO, j > 0),
                             jnp.logical_or(rows < HALO + t, j < nj - 1))
    xn_ext = xn_ref[...]
    cols = 2 * LANES
    for cg in range(c_conv // cols):
        a_val = jnp.dot(xn_ext, win_ref[:, cg * cols:(cg + 1) * cols],
                        preferred_element_type=f32)
        a_gate = jnp.dot(xn_ext, win_ref[:, c_conv + cg * cols:c_conv + (cg + 1) * cols],
                         preferred_element_type=f32)
        h = jnp.where(inside, a_val * jax.nn.sigmoid(a_gate), 0.0)
        for s in range(cols // LANES):
            h_ref[cg * (cols // LANES) + s] = h[:, s * LANES:(s + 1) * LANES]

    for g in range(n_groups):
        lanes = slice(g * LANES, (g + 1) * LANES)
        for rb in range(t // CONV_ROWS):
            acc = jnp.zeros((CONV_ROWS, LANES), f32)
            for k in range(CONV_WIDTH):
                start = rb * CONV_ROWS + HALO - CONV_PAD + k
                acc = acc + cw_ref[k:k + 1, lanes] * h_ref[g, start:start + CONV_ROWS, :]
            conv_ref[rb * CONV_ROWS:(rb + 1) * CONV_ROWS, lanes] = acc + cb_ref[:, lanes]

    xn = xn_ref[HALO:HALO + t, :]
    g_conv = jnp.dot(xn, win_ref[:, 2 * c_conv:3 * c_conv], preferred_element_type=f32)
    hc = _layer_norm_rows(conv_ref[...], clg_ref[...], clb_ref[...])
    y_ref[:, 0:c_conv] = (jax.nn.silu(hc) * jax.nn.silu(g_conv)).astype(bf16)

    base = 3 * c_conv
    u = jnp.dot(xn, win_ref[:, base:base + c_sgu], preferred_element_type=f32)
    v = jnp.dot(xn, win_ref[:, base + c_sgu:base + 2 * c_sgu], preferred_element_type=f32)
    g_sgu = jnp.dot(xn, win_ref[:, base + 2 * c_sgu:base + 3 * c_sgu],
                    preferred_element_type=f32)
    vn = _layer_norm_rows(v, slg_ref[...], slb_ref[...]).astype(bf16)
    gate = u * jax.nn.silu(g_sgu)
    for hd in range(c_sgu // LANES):
        lanes = slice(hd * LANES, (hd + 1) * LANES)
        bias = jnp.broadcast_to(bs_ref[:, hd:hd + 1], (CHUNK, LANES))
        for c in range(t // CHUNK):
            rws = slice(c * CHUNK, (c + 1) * CHUNK)
            mixed = jnp.dot(ws_ref[hd], vn[rws, lanes], preferred_element_type=f32) + bias
            y_ref[rws, c_conv + hd * LANES:c_conv + (hd + 1) * LANES] = (
                gate[rws, lanes] * mixed).astype(bf16)

    res = x + jnp.dot(y_ref[...], wout_ref[...], preferred_element_type=f32)
    o_ref[...] = _rms_norm_rows(res, fg_ref[...])


@jax.jit
def kernel(x, norm_g, w_in, conv_w, conv_b, conv_ln_g, conv_ln_b, sgu_ln_g, sgu_ln_b,
           w_s, b_s, w_out, final_g):
    depth = w_in.shape[0]
    bsz, seq, d_model = x.shape
    c_conv = conv_w.shape[-1]
    c_sgu = sgu_ln_g.shape[-1]
    d_in = w_in.shape[-1]
    d_mix = w_out.shape[1]
    n_heads = w_s.shape[1]
    t = SEQ_TILE
    assert seq % t == 0 and t % CHUNK == 0 and t % HALO == 0 and t % CONV_ROWS == 0
    assert d_in == 3 * c_conv + 3 * c_sgu and d_mix == c_conv + c_sgu
    assert c_sgu == n_heads * LANES and c_conv % (2 * LANES) == 0
    n_tiles = seq // t
    halo_per_tile = t // HALO
    n_halo_blocks = seq // HALO

    def const(shape):
        return pl.BlockSpec(shape, lambda b, j: (0,) * len(shape),
                            pipeline_mode=pl.Buffered(1))

    call = pl.pallas_call(
        _block_kernel,
        grid=(bsz, n_tiles),
        in_specs=[
            pl.BlockSpec((None, t, d_model), lambda b, j: (b, j, 0)),
            pl.BlockSpec((None, HALO, d_model),
                         lambda b, j: (b, jnp.maximum(j * halo_per_tile - 1, 0), 0)),
            pl.BlockSpec((None, HALO, d_model),
                         lambda b, j: (b, jnp.minimum((j + 1) * halo_per_tile,
                                                      n_halo_blocks - 1), 0)),
            const((1, d_model)),
            const((d_model, d_in)),
            const((CONV_WIDTH, c_conv)),
            const((1, c_conv)),
            const((1, c_conv)),
            const((1, c_conv)),
            const((1, c_sgu)),
            const((1, c_sgu)),
            const((n_heads, CHUNK, CHUNK)),
            const((CHUNK, n_heads)),
            const((d_mix, d_model)),
            const((1, d_model)),
        ],
        out_specs=pl.BlockSpec((None, t, d_model), lambda b, j: (b, j, 0)),
        out_shape=jax.ShapeDtypeStruct((bsz, seq, d_model), x.dtype),
        scratch_shapes=[
            pltpu.VMEM((t + 2 * HALO, d_model), jnp.bfloat16),
            pltpu.VMEM((c_conv // LANES, t + 2 * HALO, LANES), jnp.float32),
            pltpu.VMEM((t, c_conv), jnp.float32),
            pltpu.VMEM((t, d_mix), jnp.bfloat16),
        ],
        compiler_params=pltpu.CompilerParams(
            dimension_semantics=("arbitrary", "arbitrary"),
            vmem_limit_bytes=VMEM_LIMIT_BYTES),
        name="hybrid_block",
    )

    assert depth == 1
    row = lambda a: a.reshape(1, -1)
    return call(x, x, x, row(norm_g[0]), w_in[0].astype(jnp.bfloat16), conv_w[0],
                row(conv_b[0]), row(conv_ln_g[0]), row(conv_ln_b[0]),
                row(sgu_ln_g[0]), row(sgu_ln_b[0]), w_s[0].astype(jnp.bfloat16),
                jnp.transpose(b_s[0]), w_out[0].astype(jnp.bfloat16), row(final_g))
```

```python
import jax
import jax.numpy as jnp
from jax import lax
from jax.experimental import pallas as pl
from jax.experimental.pallas import tpu as pltpu

LANES = 128
MXU_COLS = 256
CHUNK = 128
CONV_WIDTH = 31
CONV_PAD = CONV_WIDTH // 2
HALO = 16
EPS = 1e-6
SEQ_TILE = 512
CONV_ROWS = 128
H_PAD = 8
NORM_ROWS = 32
W_IN_STAGE_ROWS = 64
W_OUT_STAGE_ROWS = 256
VMEM_LIMIT_BYTES = 56 * 1024 * 1024


def _rms_norm_rows(x, g):
    ms = jnp.mean(x * x, axis=-1, keepdims=True)
    return x * lax.rsqrt(ms + EPS) * g


def _layer_norm_rows(x, g, b):
    mu = jnp.mean(x, axis=-1, keepdims=True)
    xc = x - mu
    var = jnp.mean(xc * xc, axis=-1, keepdims=True)
    return xc * lax.rsqrt(var + EPS) * g + b


def _in_proj_column_blocks(c_conv, c_sgu):
    nc, ns = c_conv // LANES, c_sgu // LANES
    blocks = []
    for g in range(nc):
        blocks += [g, nc + g]
    blocks += [2 * nc + g for g in range(nc)]
    blocks += [3 * nc + ns + h for h in range(ns)]
    for h in range(ns):
        blocks += [3 * nc + h, 3 * nc + 2 * ns + h]
    return blocks


def _stage_weight_as_bf16(w_hbm, w_vmem, buf, sem, src_blocks):
    rows = buf.shape[1]
    n_chunks = w_hbm.shape[0] // rows

    def chunk_copy(c, slot):
        start = pl.multiple_of(c * rows, rows)
        return pltpu.make_async_copy(w_hbm.at[pl.ds(start, rows), :], buf.at[slot], sem.at[slot])

    chunk_copy(0, 0).start()

    def body(c, carry):
        slot = lax.rem(c, 2)

        @pl.when(c + 1 < n_chunks)
        def _():
            chunk_copy(c + 1, 1 - slot).start()

        chunk_copy(c, slot).wait()
        start = pl.multiple_of(c * rows, rows)
        for dst, src in enumerate(src_blocks):
            w_vmem[pl.ds(start, rows), dst * LANES:(dst + 1) * LANES] = (
                buf[slot, :, src * LANES:(src + 1) * LANES].astype(w_vmem.dtype))
        return carry

    lax.fori_loop(0, n_chunks, body, 0)


def _block_kernel(x_ref, xl_ref, xr_ref, ng_ref, win_hbm, cw_ref, cb_ref, clg_ref,
                  clb_ref, slg_ref, slb_ref, wsf_ref, bs_ref, wout_hbm, fg_ref,
                  o_ref, win_ref, wout_ref, ws_ref, win_buf, wout_buf, stage_sem,
                  cwp_ref, xn_ref, h_ref, he_ref, ho_ref, conv_ref, gc_ref, v_ref, vn_ref,
                  gate_ref, y_ref):
    t = x_ref.shape[0]
    d_model = x_ref.shape[1]
    c_conv = conv_ref.shape[1]
    c_sgu = v_ref.shape[1]
    n_groups = c_conv // LANES
    n_heads = c_sgu // LANES
    j = pl.program_id(1)
    nj = pl.num_programs(1)
    f32 = jnp.float32
    bf16 = jnp.bfloat16

    @pl.when(jnp.logical_and(pl.program_id(0) == 0, j == 0))
    def _():
        _stage_weight_as_bf16(win_hbm, win_ref, win_buf, stage_sem,
                              _in_proj_column_blocks(c_conv, c_sgu))
        _stage_weight_as_bf16(wout_hbm, wout_ref, wout_buf, stage_sem,
                              list(range(d_model // LANES)))
        ws_ref[...] = wsf_ref[...].astype(bf16)
        wbits = pltpu.bitcast(cw_ref[...].astype(bf16).astype(f32), jnp.uint32)
        cwp_ref[0:CONV_WIDTH, :] = wbits | (wbits >> 16)
        h_ref[:, HALO + t + HALO:, :] = jnp.zeros((n_groups, H_PAD, LANES), f32)

    ng = ng_ref[...]
    for src, dst0, n in ((xl_ref, 0, HALO), (x_ref, HALO, t), (xr_ref, HALO + t, HALO)):
        for r0 in range(0, n, NORM_ROWS):
            rr = min(NORM_ROWS, n - r0)
            xn_ref[dst0 + r0:dst0 + r0 + rr, :] = (
                _rms_norm_rows(src[r0:r0 + rr, :], ng).astype(bf16))

    n_ext = t + 2 * HALO
    for g in range(n_groups):
        pa = jnp.dot(xn_ref[...], win_ref[:, g * MXU_COLS:(g + 1) * MXU_COLS],
                     preferred_element_type=f32)
        h_ref[g, 0:n_ext, :] = pa[:, 0:LANES] * jax.nn.sigmoid(pa[:, LANES:MXU_COLS])
    h_ref[:, 0:HALO, :] = jnp.where(j > 0, h_ref[:, 0:HALO, :], 0.0)
    h_ref[:, HALO + t:n_ext, :] = jnp.where(j < nj - 1, h_ref[:, HALO + t:n_ext, :], 0.0)
    for g in range(n_groups):
        he_ref[g] = pltpu.bitcast(h_ref[g, 0:n_ext, :].astype(bf16), jnp.uint32)
        ho_ref[g] = pltpu.bitcast(h_ref[g, 1:n_ext + 1, :].astype(bf16), jnp.uint32)

    col = 2 * c_conv
    for cb in range(c_conv // MXU_COLS):
        cols = slice(cb * MXU_COLS, (cb + 1) * MXU_COLS)
        gc_ref[:, cols] = jax.nn.silu(jnp.dot(
            xn_ref[HALO:HALO + t, :], win_ref[:, col + cb * MXU_COLS:col + (cb + 1) * MXU_COLS],
            preferred_element_type=f32))
    col += c_conv
    for cb in range(c_sgu // MXU_COLS):
        cols = slice(cb * MXU_COLS, (cb + 1) * MXU_COLS)
        v_ref[:, cols] = jnp.dot(
            xn_ref[HALO:HALO + t, :], win_ref[:, col + cb * MXU_COLS:col + (cb + 1) * MXU_COLS],
            preferred_element_type=f32)
    col += c_sgu
    for hd in range(n_heads):
        pu = jnp.dot(xn_ref[HALO:HALO + t, :],
                     win_ref[:, col + hd * MXU_COLS:col + (hd + 1) * MXU_COLS],
                     preferred_element_type=f32)
        gate_ref[:, hd * LANES:(hd + 1) * LANES] = (
            pu[:, 0:LANES] * jax.nn.silu(pu[:, LANES:MXU_COLS]))

    half = CONV_ROWS // 2
    for g in range(n_groups):
        lanes = slice(g * LANES, (g + 1) * LANES)
        for rb in range(t // CONV_ROWS):
            acc = None
            for k in range(CONV_WIDTH):
                start = rb * CONV_ROWS + HALO - CONV_PAD + k
                src = ho_ref if start % 2 else he_ref
                win = pltpu.bitcast(src[g, start // 2:start // 2 + half, :], bf16)
                wk = pltpu.bitcast(jnp.broadcast_to(cwp_ref[k:k + 1, lanes], (half, LANES)), bf16)
                acc = wk * win if acc is None else acc + wk * win
            conv_ref[rb * CONV_ROWS:(rb + 1) * CONV_ROWS, lanes] = (
                acc.astype(f32) + cb_ref[:, lanes])

    for r0 in range(0, t, NORM_ROWS):
        rws = slice(r0, r0 + NORM_ROWS)
        vn_ref[rws, :] = _layer_norm_rows(v_ref[rws, :], slg_ref[...], slb_ref[...]).astype(bf16)
        hc = _layer_norm_rows(conv_ref[rws, :], clg_ref[...], clb_ref[...])
        y_ref[rws, 0:c_conv] = (jax.nn.silu(hc) * gc_ref[rws, :]).astype(bf16)

    for hd in range(n_heads):
        lanes = slice(hd * LANES, (hd + 1) * LANES)
        bias = jnp.broadcast_to(bs_ref[:, hd:hd + 1], (CHUNK, LANES))
        for c in range(t // CHUNK):
            rws = slice(c * CHUNK, (c + 1) * CHUNK)
            mixed = jnp.dot(ws_ref[hd], vn_ref[rws, lanes], preferred_element_type=f32) + bias
            y_ref[rws, c_conv + hd * LANES:c_conv + (hd + 1) * LANES] = (
                gate_ref[rws, lanes] * mixed).astype(bf16)

    for cb in range(d_model // MXU_COLS):
        cols = slice(cb * MXU_COLS, (cb + 1) * MXU_COLS)
        o_ref[:, cols] = x_ref[:, cols] + jnp.dot(y_ref[...], wout_ref[:, cols],
                                                  preferred_element_type=f32)
    fg = fg_ref[...]
    for r0 in range(0, t, NORM_ROWS):
        rws = slice(r0, r0 + NORM_ROWS)
        o_ref[rws, :] = _rms_norm_rows(o_ref[rws, :], fg)


@jax.jit
def kernel(x, norm_g, w_in, conv_w, conv_b, conv_ln_g, conv_ln_b, sgu_ln_g, sgu_ln_b,
           w_s, b_s, w_out, final_g):
    depth = w_in.shape[0]
    bsz, seq, d_model = x.shape
    c_conv = conv_w.shape[-1]
    c_sgu = sgu_ln_g.shape[-1]
    d_in = w_in.shape[-1]
    d_mix = w_out.shape[1]
    n_heads = w_s.shape[1]
    t = SEQ_TILE
    assert seq % t == 0 and t % CHUNK == 0 and t % CONV_ROWS == 0 and t % NORM_ROWS == 0
    assert d_in == 3 * c_conv + 3 * c_sgu and d_mix == c_conv + c_sgu
    assert c_sgu == n_heads * LANES and c_conv % MXU_COLS == 0 and d_model % MXU_COLS == 0
    assert d_model % W_IN_STAGE_ROWS == 0 and d_mix % W_OUT_STAGE_ROWS == 0
    assert CONV_WIDTH <= 2 * HALO and CONV_PAD <= HALO
    n_tiles = seq // t
    halo_per_tile = t // HALO
    n_halo_blocks = seq // HALO

    def const(shape):
        return pl.BlockSpec(shape, lambda b, j: (0,) * len(shape),
                            pipeline_mode=pl.Buffered(1))

    call = pl.pallas_call(
        _block_kernel,
        grid=(bsz, n_tiles),
        in_specs=[
            pl.BlockSpec((None, t, d_model), lambda b, j: (b, j, 0)),
            pl.BlockSpec((None, HALO, d_model),
                         lambda b, j: (b, jnp.maximum(j * halo_per_tile - 1, 0), 0)),
            pl.BlockSpec((None, HALO, d_model),
                         lambda b, j: (b, jnp.minimum((j + 1) * halo_per_tile,
                                                      n_halo_blocks - 1), 0)),
            const((1, d_model)),
            pl.BlockSpec(memory_space=pl.ANY),
            const((CONV_WIDTH, c_conv)),
            const((1, c_conv)),
            const((1, c_conv)),
            const((1, c_conv)),
            const((1, c_sgu)),
            const((1, c_sgu)),
            const((n_heads, CHUNK, CHUNK)),
            const((CHUNK, n_heads)),
            pl.BlockSpec(memory_space=pl.ANY),
            const((1, d_model)),
        ],
        out_specs=pl.BlockSpec((None, t, d_model), lambda b, j: (b, j, 0)),
        out_shape=jax.ShapeDtypeStruct((bsz, seq, d_model), x.dtype),
        scratch_shapes=[
            pltpu.VMEM((d_model, d_in), jnp.bfloat16),
            pltpu.VMEM((d_mix, d_model), jnp.bfloat16),
            pltpu.VMEM((n_heads, CHUNK, CHUNK), jnp.bfloat16),
            pltpu.VMEM((2, W_IN_STAGE_ROWS, d_in), jnp.float32),
            pltpu.VMEM((2, W_OUT_STAGE_ROWS, d_model), jnp.float32),
            pltpu.SemaphoreType.DMA((2,)),
            pltpu.VMEM((2 * HALO, c_conv), jnp.uint32),
            pltpu.VMEM((t + 2 * HALO, d_model), jnp.bfloat16),
            pltpu.VMEM((c_conv // LANES, t + 2 * HALO + H_PAD, LANES), jnp.float32),
            pltpu.VMEM((c_conv // LANES, t // 2 + HALO, LANES), jnp.uint32),
            pltpu.VMEM((c_conv // LANES, t // 2 + HALO, LANES), jnp.uint32),
            pltpu.VMEM((t, c_conv), jnp.float32),
            pltpu.VMEM((t, c_conv), jnp.float32),
            pltpu.VMEM((t, c_sgu), jnp.float32),
            pltpu.VMEM((t, c_sgu), jnp.bfloat16),
            pltpu.VMEM((t, c_sgu), jnp.float32),
            pltpu.VMEM((t, d_mix), jnp.bfloat16),
        ],
        compiler_params=pltpu.CompilerParams(
            dimension_semantics=("arbitrary", "arbitrary"),
            vmem_limit_bytes=VMEM_LIMIT_BYTES),
        name="hybrid_block",
    )

    assert depth == 1
    row = lambda a: a.reshape(1, -1)
    return call(x, x, x, row(norm_g[0]), w_in[0], conv_w[0], row(conv_b[0]),
                row(conv_ln_g[0]), row(conv_ln_b[0]), row(sgu_ln_g[0]), row(sgu_ln_b[0]),
                w_s[0], jnp.transpose(b_s[0]), w_out[0], row(final_g))
```

```python
import functools

import jax
import jax.numpy as jnp
from jax import lax
from jax.experimental import pallas as pl
from jax.experimental.pallas import tpu as pltpu

LANES = 128
MXU_COLS = 256
CHUNK = 128
CONV_WIDTH = 31
CONV_PAD = CONV_WIDTH // 2
HALO = 16
EPS = 1e-6
SEQ_TILE = 512
CONV_ROWS = 128
H_PAD = 8
NORM_ROWS = 32
W_IN_STAGE_ROWS = 64
W_OUT_STAGE_ROWS = 256
VMEM_LIMIT_BYTES = 56 * 1024 * 1024


def _rms_norm_rows(x, g):
    ms = jnp.mean(x * x, axis=-1, keepdims=True)
    return x * lax.rsqrt(ms + EPS) * g


def _layer_norm_rows(x, g, b):
    mu = jnp.mean(x, axis=-1, keepdims=True)
    xc = x - mu
    var = jnp.mean(xc * xc, axis=-1, keepdims=True)
    return xc * lax.rsqrt(var + EPS) * g + b


def _in_proj_column_blocks(c_conv, c_sgu):
    nc, ns = c_conv // LANES, c_sgu // LANES
    blocks = []
    for g in range(nc):
        blocks += [g, nc + g]
    blocks += [2 * nc + g for g in range(nc)]
    blocks += [3 * nc + ns + h for h in range(ns)]
    for h in range(ns):
        blocks += [3 * nc + h, 3 * nc + 2 * ns + h]
    return blocks


def _stage_weight_as_bf16(w_hbm, w_vmem, buf, sem, src_blocks):
    rows = buf.shape[1]
    n_chunks = w_hbm.shape[0] // rows

    def chunk_copy(c, slot):
        start = pl.multiple_of(c * rows, rows)
        return pltpu.make_async_copy(w_hbm.at[pl.ds(start, rows), :], buf.at[slot], sem.at[slot])

    chunk_copy(0, 0).start()

    def body(c, carry):
        slot = lax.rem(c, 2)

        @pl.when(c + 1 < n_chunks)
        def _():
            chunk_copy(c + 1, 1 - slot).start()

        chunk_copy(c, slot).wait()
        start = pl.multiple_of(c * rows, rows)
        for dst, src in enumerate(src_blocks):
            w_vmem[pl.ds(start, rows), dst * LANES:(dst + 1) * LANES] = (
                buf[slot, :, src * LANES:(src + 1) * LANES].astype(w_vmem.dtype))
        return carry

    lax.fori_loop(0, n_chunks, body, 0)


def _final_norm(res_ref, fg_ref, o_ref):
    fg = fg_ref[...]
    for r0 in range(0, res_ref.shape[0], NORM_ROWS):
        rws = slice(r0, r0 + NORM_ROWS)
        o_ref[rws, :] = _rms_norm_rows(res_ref[rws, :], fg)


def _input_norm(xn_ref, dst0, src_ref, src0, n, ng):
    for r0 in range(0, n, NORM_ROWS):
        rr = min(NORM_ROWS, n - r0)
        xn_ref[dst0 + r0:dst0 + r0 + rr, :] = (
            _rms_norm_rows(src_ref[src0 + r0:src0 + r0 + rr, :], ng).astype(jnp.bfloat16))


def _block_kernel(tiles_per_seq, n_tiles, x_ref, xnext_ref, xr_ref, ng_ref, win_hbm, cw_ref,
                  cb_ref, clg_ref, clb_ref, slg_ref, slb_ref, wsf_ref, bs_ref, wout_hbm,
                  fg_ref, o_ref, win_ref, wout_ref, ws_ref, win_buf, wout_buf, stage_sem,
                  cwp_ref, xn_ref, h_ref, he_ref, ho_ref, conv_ref, gc_ref, v_ref, vn_ref,
                  gate_ref, y_ref, res_ref):
    t = x_ref.shape[0]
    c_conv = conv_ref.shape[1]
    c_sgu = v_ref.shape[1]
    n_groups = c_conv // LANES
    s = pl.program_id(0)
    f32 = jnp.float32
    bf16 = jnp.bfloat16
    refs = (x_ref, xnext_ref, xr_ref, ng_ref, cw_ref, cb_ref, clg_ref, clb_ref, slg_ref,
            slb_ref, bs_ref, fg_ref, o_ref, win_ref, wout_ref, ws_ref, cwp_ref, xn_ref, h_ref,
            he_ref, ho_ref, conv_ref, gc_ref, v_ref, vn_ref, gate_ref, y_ref, res_ref)

    @pl.when(s == 0)
    def _():
        _stage_weight_as_bf16(win_hbm, win_ref, win_buf, stage_sem,
                              _in_proj_column_blocks(c_conv, c_sgu))
        _stage_weight_as_bf16(wout_hbm, wout_ref, wout_buf, stage_sem,
                              list(range(wout_ref.shape[1] // LANES)))
        ws_ref[...] = wsf_ref[...].astype(bf16)
        wbits = pltpu.bitcast(cw_ref[...].astype(bf16).astype(f32), jnp.uint32)
        cwp_ref[0:CONV_WIDTH, :] = wbits | (wbits >> 16)
        h_ref[:, HALO + t + HALO:, :] = jnp.zeros((n_groups, H_PAD, LANES), f32)
        ng = ng_ref[...]
        xn_ref[0:HALO, :] = jnp.zeros((HALO, xn_ref.shape[1]), bf16)
        _input_norm(xn_ref, HALO, x_ref, 0, t, ng)
        _input_norm(xn_ref, HALO + t, xnext_ref, 0, HALO, ng)
        res_ref[...] = jnp.zeros(res_ref.shape, f32)

    j = lax.rem(jnp.minimum(s, n_tiles - 1), tiles_per_seq)
    pl.when(s < n_tiles)(lambda: _tile_step(j == 0, j == tiles_per_seq - 1, *refs))
    pl.when(s == n_tiles)(lambda: _final_norm(res_ref, fg_ref, o_ref))


def _tile_step(first_in_seq, last_in_seq, x_ref, xnext_ref, xr_ref, ng_ref, cw_ref, cb_ref,
               clg_ref, clb_ref, slg_ref, slb_ref, bs_ref, fg_ref, o_ref, win_ref, wout_ref,
               ws_ref, cwp_ref, xn_ref, h_ref, he_ref, ho_ref, conv_ref, gc_ref, v_ref, vn_ref,
               gate_ref, y_ref, res_ref):
    t = x_ref.shape[0]
    d_model = x_ref.shape[1]
    c_conv = conv_ref.shape[1]
    c_sgu = v_ref.shape[1]
    n_groups = c_conv // LANES
    n_heads = c_sgu // LANES
    f32 = jnp.float32
    bf16 = jnp.bfloat16

    _final_norm(res_ref, fg_ref, o_ref)

    n_ext = t + 2 * HALO
    for g in range(n_groups):
        pa = jnp.dot(xn_ref[...], win_ref[:, g * MXU_COLS:(g + 1) * MXU_COLS],
                     preferred_element_type=f32)
        hg = pa[:, 0:LANES] * jax.nn.sigmoid(pa[:, LANES:MXU_COLS])
        h_ref[g, 0:HALO, :] = jnp.where(first_in_seq, 0.0, hg[0:HALO])
        h_ref[g, HALO:HALO + t, :] = hg[HALO:HALO + t]
        h_ref[g, HALO + t:n_ext, :] = jnp.where(last_in_seq, 0.0, hg[HALO + t:n_ext])
        he_ref[g] = pltpu.bitcast(h_ref[g, 0:n_ext, :].astype(bf16), jnp.uint32)
        ho_ref[g] = pltpu.bitcast(h_ref[g, 1:n_ext + 1, :].astype(bf16), jnp.uint32)

    col = 2 * c_conv
    for cb in range(c_conv // MXU_COLS):
        cols = slice(cb * MXU_COLS, (cb + 1) * MXU_COLS)
        gc_ref[:, cols] = jax.nn.silu(jnp.dot(
            xn_ref[HALO:HALO + t, :], win_ref[:, col + cb * MXU_COLS:col + (cb + 1) * MXU_COLS],
            preferred_element_type=f32)).astype(bf16)
    col += c_conv
    for cb in range(c_sgu // MXU_COLS):
        cols = slice(cb * MXU_COLS, (cb + 1) * MXU_COLS)
        v_ref[:, cols] = jnp.dot(
            xn_ref[HALO:HALO + t, :], win_ref[:, col + cb * MXU_COLS:col + (cb + 1) * MXU_COLS],
            preferred_element_type=f32)
    col += c_sgu
    for hd in range(n_heads):
        pu = jnp.dot(xn_ref[HALO:HALO + t, :],
                     win_ref[:, col + hd * MXU_COLS:col + (hd + 1) * MXU_COLS],
                     preferred_element_type=f32)
        gate_ref[:, hd * LANES:(hd + 1) * LANES] = (
            pu[:, 0:LANES] * jax.nn.silu(pu[:, LANES:MXU_COLS])).astype(bf16)

    half = CONV_ROWS // 2
    for g in range(n_groups):
        lanes = slice(g * LANES, (g + 1) * LANES)
        for rb in range(t // CONV_ROWS):
            acc = None
            for k in range(CONV_WIDTH):
                start = rb * CONV_ROWS + HALO - CONV_PAD + k
                src = ho_ref if start % 2 else he_ref
                win = pltpu.bitcast(src[g, start // 2:start // 2 + half, :], bf16)
                wk = pltpu.bitcast(jnp.broadcast_to(cwp_ref[k:k + 1, lanes], (half, LANES)), bf16)
                acc = wk * win if acc is None else acc + wk * win
            conv_ref[rb * CONV_ROWS:(rb + 1) * CONV_ROWS, lanes] = acc

    for r0 in range(0, t, NORM_ROWS):
        rws = slice(r0, r0 + NORM_ROWS)
        vn_ref[rws, :] = _layer_norm_rows(v_ref[rws, :], slg_ref[...], slb_ref[...]).astype(bf16)
        hc = _layer_norm_rows(conv_ref[rws, :].astype(f32) + cb_ref[...], clg_ref[...], clb_ref[...])
        y_ref[rws, 0:c_conv] = (jax.nn.silu(hc) * gc_ref[rws, :]).astype(bf16)

    for hd in range(n_heads):
        lanes = slice(hd * LANES, (hd + 1) * LANES)
        bias = jnp.broadcast_to(bs_ref[:, hd:hd + 1], (CHUNK, LANES))
        for c in range(t // CHUNK):
            rws = slice(c * CHUNK, (c + 1) * CHUNK)
            mixed = jnp.dot(ws_ref[hd], vn_ref[rws, lanes], preferred_element_type=f32) + bias
            y_ref[rws, c_conv + hd * LANES:c_conv + (hd + 1) * LANES] = (
                gate_ref[rws, lanes] * mixed).astype(bf16)

    for cb in range(d_model // MXU_COLS):
        cols = slice(cb * MXU_COLS, (cb + 1) * MXU_COLS)
        res_ref[:, cols] = x_ref[:, cols] + jnp.dot(y_ref[...], wout_ref[:, cols],
                                                    preferred_element_type=f32)

    ng = ng_ref[...]
    _input_norm(xn_ref, 0, x_ref, t - HALO, HALO, ng)
    _input_norm(xn_ref, HALO, xnext_ref, 0, t, ng)
    _input_norm(xn_ref, HALO + t, xr_ref, 0, HALO, ng)


@jax.jit
def kernel(x, norm_g, w_in, conv_w, conv_b, conv_ln_g, conv_ln_b, sgu_ln_g, sgu_ln_b,
           w_s, b_s, w_out, final_g):
    depth = w_in.shape[0]
    bsz, seq, d_model = x.shape
    c_conv = conv_w.shape[-1]
    c_sgu = sgu_ln_g.shape[-1]
    d_in = w_in.shape[-1]
    d_mix = w_out.shape[1]
    n_heads = w_s.shape[1]
    t = SEQ_TILE
    assert seq % t == 0 and seq > t and t % CHUNK == 0 and t % CONV_ROWS == 0 and t % NORM_ROWS == 0
    assert d_in == 3 * c_conv + 3 * c_sgu and d_mix == c_conv + c_sgu
    assert c_sgu == n_heads * LANES and c_conv % MXU_COLS == 0 and d_model % MXU_COLS == 0
    assert d_model % W_IN_STAGE_ROWS == 0 and d_mix % W_OUT_STAGE_ROWS == 0
    assert CONV_WIDTH <= 2 * HALO and CONV_PAD <= HALO
    tiles_per_seq = seq // t
    n_tiles = bsz * tiles_per_seq
    halo_per_tile = t // HALO
    n_halo_blocks = seq // HALO

    def const(shape):
        return pl.BlockSpec(shape, lambda s: (0,) * len(shape), pipeline_mode=pl.Buffered(1))

    def tile_map(offset):
        def index_map(s):
            tile = jnp.clip(s + offset, 0, n_tiles - 1)
            return tile // tiles_per_seq, tile % tiles_per_seq, 0
        return index_map

    def next_right_halo_map(s):
        tile = jnp.minimum(s + 1, n_tiles - 1)
        halo = jnp.minimum((tile % tiles_per_seq + 1) * halo_per_tile, n_halo_blocks - 1)
        return tile // tiles_per_seq, halo, 0

    call = pl.pallas_call(
        functools.partial(_block_kernel, tiles_per_seq, n_tiles),
        grid=(n_tiles + 1,),
        in_specs=[
            pl.BlockSpec((None, t, d_model), tile_map(0)),
            pl.BlockSpec((None, t, d_model), tile_map(1)),
            pl.BlockSpec((None, HALO, d_model), next_right_halo_map),
            const((1, d_model)),
            pl.BlockSpec(memory_space=pl.ANY),
            const((CONV_WIDTH, c_conv)),
            const((1, c_conv)),
            const((1, c_conv)),
            const((1, c_conv)),
            const((1, c_sgu)),
            const((1, c_sgu)),
            const((n_heads, CHUNK, CHUNK)),
            const((CHUNK, n_heads)),
            pl.BlockSpec(memory_space=pl.ANY),
            const((1, d_model)),
        ],
        out_specs=pl.BlockSpec((None, t, d_model), tile_map(-1)),
        out_shape=jax.ShapeDtypeStruct((bsz, seq, d_model), x.dtype),
        scratch_shapes=[
            pltpu.VMEM((d_model, d_in), jnp.bfloat16),
            pltpu.VMEM((d_mix, d_model), jnp.bfloat16),
            pltpu.VMEM((n_heads, CHUNK, CHUNK), jnp.bfloat16),
            pltpu.VMEM((2, W_IN_STAGE_ROWS, d_in), jnp.float32),
            pltpu.VMEM((2, W_OUT_STAGE_ROWS, d_model), jnp.float32),
            pltpu.SemaphoreType.DMA((2,)),
            pltpu.VMEM((2 * HALO, c_conv), jnp.uint32),
            pltpu.VMEM((t + 2 * HALO, d_model), jnp.bfloat16),
            pltpu.VMEM((c_conv // LANES, t + 2 * HALO + H_PAD, LANES), jnp.float32),
            pltpu.VMEM((c_conv // LANES, t // 2 + HALO, LANES), jnp.uint32),
            pltpu.VMEM((c_conv // LANES, t // 2 + HALO, LANES), jnp.uint32),
            pltpu.VMEM((t, c_conv), jnp.bfloat16),
            pltpu.VMEM((t, c_conv), jnp.bfloat16),
            pltpu.VMEM((t, c_sgu), jnp.float32),
            pltpu.VMEM((t, c_sgu), jnp.bfloat16),
            pltpu.VMEM((t, c_sgu), jnp.bfloat16),
            pltpu.VMEM((t, d_mix), jnp.bfloat16),
            pltpu.VMEM((t, d_model), jnp.float32),
        ],
        compiler_params=pltpu.CompilerParams(
            dimension_semantics=("arbitrary",),
            vmem_limit_bytes=VMEM_LIMIT_BYTES),
        name="hybrid_block",
    )

    assert depth == 1
    row = lambda a: a.reshape(1, -1)
    return call(x, x, x, row(norm_g[0]), w_in[0], conv_w[0], row(conv_b[0]),
                row(conv_ln_g[0]), row(conv_ln_b[0]), row(sgu_ln_g[0]), row(sgu_ln_b[0]),
                w_s[0], jnp.transpose(b_s[0]), w_out[0], row(final_g))
```
